```python
import math
import jax, jax.numpy as jnp
from jax import lax
import numpy as np


D_MODEL = 1024
BATCH = 8
SEQ = 4096
DEPTH = 1

CHUNK = 64
MIX_WIDTH = D_MODEL
RET_WIDTH = MIX_WIDTH // 2
CONV_WIDTH_CH = MIX_WIDTH - RET_WIDTH
RET_HEADS = 4
RET_HEAD_DIM = RET_WIDTH // RET_HEADS
CONV_GROUPS = 4
CONV_K = 3
D_FF = 2816
ROPE_BASE = 10000.0
N_IN_COLS = 4 * RET_WIDTH + 3 * CONV_WIDTH_CH
N_ADA = 6
EPS = 1e-6

kernel_name = 'hybrid_retention_shortconv_convffn_adaln'


def rms_norm(x, g):
    xf = x.astype(jnp.float32)
    y = xf * lax.rsqrt(jnp.mean(xf * xf, axis=-1, keepdims=True) + EPS)
    return (y * g.astype(jnp.float32)).astype(x.dtype)


def modulate(h, shift, scale):
    return h * (1.0 + scale[:, None, :]) + shift[:, None, :]


def causal_dwconv(x, w, b):
    ch = x.shape[-1]
    y = lax.conv_general_dilated(
        x, w[:, None, :].astype(x.dtype), window_strides=(1,),
        padding=[(CONV_K - 1, 0)], dimension_numbers=('NWC', 'WIO', 'NWC'),
        feature_group_count=ch)
    return y + b.astype(x.dtype)


def rotary(x, positions):
    dh = x.shape[-1]
    inv_freq = ROPE_BASE ** (-jnp.arange(0, dh, 2, dtype=jnp.float32) / dh)
    ang = positions.astype(jnp.float32)[..., None] * inv_freq
    cos = jnp.cos(ang)[:, :, None, :].astype(x.dtype)
    sin = jnp.sin(ang)[:, :, None, :].astype(x.dtype)
    x1, x2 = jnp.split(x, 2, axis=-1)
    return jnp.concatenate([x1 * cos - x2 * sin, x2 * cos + x1 * sin], axis=-1)


def chunk_retention(q, k, v, log_gamma):
    b, s, h, dk = q.shape
    dv = v.shape[-1]
    n = s // CHUNK
    q = q.reshape(b, n, CHUNK, h, dk).astype(jnp.float32) * (dk ** -0.5)
    k = k.reshape(b, n, CHUNK, h, dk).astype(jnp.float32)
    v = v.reshape(b, n, CHUNK, h, dv).astype(jnp.float32)
    j = jnp.arange(CHUNK, dtype=jnp.float32)
    intra_decay = jnp.exp(jnp.abs(j[:, None] - j[None, :])[None] * log_gamma[:, None, None])
    scores = jnp.einsum('bnihd,bnjhd->bnhij', q, k) * intra_decay
    intra = jnp.einsum('bnhij,bnjhv->bnihv', scores, v)
    k_w = jnp.exp((CHUNK - 1 - j)[:, None] * log_gamma[None, :])
    kv = jnp.einsum('bnjhd,jh,bnjhv->nbhdv', k, k_w, v)
    chunk_decay = jnp.exp(CHUNK * log_gamma)[None, :, None, None]

    def step(state, kv_n):
        return state * chunk_decay + kv_n, state

    _, prev = lax.scan(step, jnp.zeros((b, h, dk, dv), jnp.float32), kv)
    q_w = jnp.exp((j + 1.0)[:, None] * log_gamma[None, :])
    cross = jnp.einsum('bnihd,ih,nbhdv->bnihv', q, q_w, prev)
    return (intra + cross).reshape(b, s, h, dv)


def setup_inputs(seed: int = 0) -> dict:
    key = jax.random.key(seed)
    ks = jax.random.split(key, 20)
    f32 = jnp.float32
    nrm = lambda k, shape, scale: jax.random.normal(k, shape, f32) * scale
    x = jax.random.normal(ks[0], (BATCH, SEQ, D_MODEL), f32)
    c = jax.random.normal(ks[1], (BATCH, D_MODEL), f32)
    positions = jnp.broadcast_to(jnp.arange(SEQ, dtype=jnp.int32)[None, :], (BATCH, SEQ))
    return {
        'x': x,
        'c': c,
        'positions': positions,
        'w_ada': nrm(ks[2], (D_MODEL, N_ADA * D_MODEL), D_MODEL ** -0.5),
        'b_ada': nrm(ks[3], (N_ADA * D_MODEL,), 0.02),
        'norm1_g': 1.0 + nrm(ks[4], (D_MODEL,), 0.02),
        'w_in': nrm(ks[5], (D_MODEL, N_IN_COLS), D_MODEL ** -0.5),
        'conv_mix_w': nrm(ks[6], (CONV_K, CONV_WIDTH_CH), CONV_K ** -0.5),
        'conv_mix_b': nrm(ks[7], (CONV_WIDTH_CH,), 0.02),
        'ret_norm_g': 1.0 + nrm(ks[8], (RET_WIDTH,), 0.02),
        'conv_norm_g': 1.0 + nrm(ks[9], (CONV_WIDTH_CH,), 0.02),
        'w_out': nrm(ks[10], (MIX_WIDTH, D_MODEL), MIX_WIDTH ** -0.5),
        'norm2_g': 1.0 + nrm(ks[11], (D_MODEL,), 0.02),
        'w_up': nrm(ks[12], (D_MODEL, 2 * D_FF), D_MODEL ** -0.5),
        'conv_ffn_w': nrm(ks[13], (CONV_K, 2 * D_FF), CONV_K ** -0.5),
        'conv_ffn_b': nrm(ks[14], (2 * D_FF,), 0.02),
        'w_down': nrm(ks[15], (D_FF, D_MODEL), D_FF ** -0.5),
        'final_g': 1.0 + nrm(ks[16], (D_MODEL,), 0.02),
    }


def reference(x, c, positions, w_ada, b_ada, norm1_g, w_in, conv_mix_w, conv_mix_b,
              ret_norm_g, conv_norm_g, w_out, norm2_g, w_up, conv_ffn_w, conv_ffn_b,
              w_down, final_g):
    b, s, _ = x.shape
    log_gamma = jnp.log(1.0 - 2.0 ** (-5.0 - jnp.arange(RET_HEADS, dtype=jnp.float32)))
    for _layer in range(DEPTH):
        mod = jax.nn.silu(c) @ w_ada + b_ada
        sh1, sc1, gt1, sh2, sc2, gt2 = jnp.split(mod, N_ADA, axis=-1)

        h = modulate(rms_norm(x, norm1_g), sh1, sc1)
        proj = h @ w_in
        q, k, v, g, hc, gb, gc = jnp.split(proj, 7, axis=-1)

        q = rotary(q.reshape(b, s, RET_HEADS, RET_HEAD_DIM), positions)
        k = rotary(k.reshape(b, s, RET_HEADS, RET_HEAD_DIM), positions)
        v = v.reshape(b, s, RET_HEADS, RET_HEAD_DIM)
        o = chunk_retention(q, k, v, log_gamma)
        mu = jnp.mean(o, axis=-1, keepdims=True)
        var = jnp.mean(jnp.square(o - mu), axis=-1, keepdims=True)
        o = ((o - mu) * lax.rsqrt(var + EPS)).reshape(b, s, RET_WIDTH)
        ret_out = jax.nn.silu(g) * (o * ret_norm_g).astype(x.dtype)

        conv_y = gb * causal_dwconv(gc * hc, conv_mix_w, conv_mix_b)
        conv_out = rms_norm(conv_y, conv_norm_g)

        mixed = jnp.concatenate([ret_out, conv_out], axis=-1) @ w_out
        x = x + gt1[:, None, :] * mixed

        h = modulate(rms_norm(x, norm2_g), sh2, sc2)
        u = causal_dwconv(h @ w_up, conv_ffn_w, conv_ffn_b)
        ua, ub = jnp.split(u, 2, axis=-1)
        x = x + gt2[:, None, :] * ((jax.nn.silu(ua) * ub) @ w_down)
    return rms_norm(x, final_g)
```

```python
import functools

import numpy as np
import jax
import jax.numpy as jnp
from jax import lax
from jax.experimental import pallas as pl
from jax.experimental.pallas import tpu as pltpu

D_MODEL = 1024
RET_WIDTH = 512
CONV_WIDTH = 512
RET_HEADS = 4
HEAD_DIM = 128
CONV_K = 3
D_FF = 2816
N_IN_COLS = 4 * RET_WIDTH + 3 * CONV_WIDTH
N_ADA = 6
ROPE_BASE = 10000.0
EPS = 1e-6

SUBLANES = 8
SEQ_TILE = 256
RET_BLOCK = 256
REF_CHUNK = 64
FF_CHUNK = 256
VMEM_LIMIT_BYTES = 56 * 1024 * 1024

_LOG_GAMMA = [float(np.log(np.float32(1.0) - np.float32(2.0) ** np.float32(-5.0 - h)))
              for h in range(RET_HEADS)]


def _silu(v):
    return v * (1.0 / (1.0 + jnp.exp(-v)))


def _bdot(a, b):
    return jnp.dot(a.astype(jnp.bfloat16), b.astype(jnp.bfloat16),
                   preferred_element_type=jnp.float32)


def _ada_kernel(c_ref, w_ref, b_ref, o_ref):
    o_ref[...] = _bdot(_silu(c_ref[...]), w_ref[...]) + b_ref[...]


def _ada(c, w_ada, b_ada):
    bsz = c.shape[0]
    return pl.pallas_call(
        _ada_kernel,
        grid=(N_ADA,),
        in_specs=[
            pl.BlockSpec((bsz, D_MODEL), lambda j: (0, 0)),
            pl.BlockSpec((D_MODEL, D_MODEL), lambda j: (0, j)),
            pl.BlockSpec((1, D_MODEL), lambda j: (0, j)),
        ],
        out_specs=pl.BlockSpec((bsz, D_MODEL), lambda j: (0, j)),
        out_shape=jax.ShapeDtypeStruct((bsz, N_ADA * D_MODEL), jnp.float32),
        compiler_params=pltpu.CompilerParams(dimension_semantics=("arbitrary",)),
        name="ada",
    )(c, w_ada, b_ada.reshape(1, -1))


def _mixer_kernel(x_ref, pos_ref, mod_ref, n1g_ref, w_in_ref, invf_ref, cw_ref, cb_ref,
                  rg_ref, cg_ref, w_out_ref, o_ref,
                  state_ref, zbuf_ref, decay_ref, qw_ref, kw_ref):
    b = pl.program_id(0)
    s = pl.program_id(1)
    ts = x_ref.shape[1]
    blk = RET_BLOCK

    @pl.when(jnp.logical_and(b == 0, s == 0))
    def _init_tables():
        ii = lax.broadcasted_iota(jnp.int32, (blk, blk), 0)
        jj = lax.broadcasted_iota(jnp.int32, (blk, blk), 1)
        dist = jnp.abs(ii - jj).astype(jnp.float32)
        visible = (jj // REF_CHUNK) <= (ii // REF_CHUNK)
        row = lax.broadcasted_iota(jnp.int32, (blk, HEAD_DIM), 0).astype(jnp.float32)
        for h in range(RET_HEADS):
            lg = _LOG_GAMMA[h]
            decay_ref[h] = jnp.where(visible, jnp.exp(dist * lg), 0.0) * (HEAD_DIM ** -0.5)
            qw_ref[h] = jnp.exp((row + 1.0) * lg) * (HEAD_DIM ** -0.5)
            kw_ref[h] = jnp.exp((blk - 1.0 - row) * lg)

    @pl.when(s == 0)
    def _reset():
        state_ref[...] = jnp.zeros_like(state_ref)
        zbuf_ref[0:SUBLANES, :] = jnp.zeros((SUBLANES, CONV_WIDTH), jnp.float32)

    x = x_ref[0]
    sh1 = mod_ref[0, 0:1, :]
    sc1 = mod_ref[0, 1:2, :]
    gt1 = mod_ref[0, 2:3, :]
    r = lax.rsqrt(jnp.mean(x * x, axis=-1, keepdims=True) + EPS)
    h = ((x * r) * (n1g_ref[...] * (1.0 + sc1)) + sh1).astype(jnp.bfloat16)

    def proj(i):
        return jnp.dot(h, w_in_ref[:, i * RET_WIDTH:(i + 1) * RET_WIDTH],
                       preferred_element_type=jnp.float32)

    q = proj(0)
    k = proj(1)
    v = proj(2)
    g = proj(3)
    hc = proj(4)
    gb = proj(5)
    gc = proj(6)

    ang = pos_ref[0].astype(jnp.float32) * invf_ref[...]
    cos = jnp.cos(ang)
    lane = lax.broadcasted_iota(jnp.int32, (1, HEAD_DIM), 1)
    sin = jnp.sin(ang) * jnp.where(lane < HEAD_DIM // 2, -1.0, 1.0)

    def rot(t):
        return t * cos + pltpu.roll(t, HEAD_DIM // 2, 1) * sin

    ret_parts = []
    for hh in range(RET_HEADS):
        cs = slice(hh * HEAD_DIM, (hh + 1) * HEAD_DIM)
        qh = rot(q[:, cs])
        kh = rot(k[:, cs])
        vh = v[:, cs]
        o_blocks = []
        for nb in range(ts // blk):
            rs = slice(nb * blk, (nb + 1) * blk)
            qb = qh[rs].astype(jnp.bfloat16)
            kb = kh[rs]
            vb = vh[rs].astype(jnp.bfloat16)
            sc = lax.dot_general(qb, kb.astype(jnp.bfloat16), (((1,), (1,)), ((), ())),
                                 preferred_element_type=jnp.float32)
            intra = _bdot(sc * decay_ref[hh], vb)
            st = state_ref[hh]
            cross = _bdot(qb, st) * qw_ref[hh]
            kv = lax.dot_general((kb * kw_ref[hh]).astype(jnp.bfloat16), vb,
                                 (((0,), (0,)), ((), ())),
                                 preferred_element_type=jnp.float32)
            state_ref[hh] = st * float(np.exp(blk * _LOG_GAMMA[hh])) + kv
            o_blocks.append(intra + cross)
        o = o_blocks[0] if len(o_blocks) == 1 else jnp.concatenate(o_blocks, axis=0)
        mu = jnp.mean(o, axis=-1, keepdims=True)
        oc = o - mu
        var = jnp.mean(oc * oc, axis=-1, keepdims=True)
        on = oc * lax.rsqrt(var + EPS)
        ret_parts.append(_silu(g[:, cs]) * (on * rg_ref[:, cs]))

    z = gc * hc
    zbuf_ref[SUBLANES:SUBLANES + ts, :] = z
    z1 = zbuf_ref[SUBLANES - 1:SUBLANES - 1 + ts, :]
    z2 = zbuf_ref[SUBLANES - 2:SUBLANES - 2 + ts, :]
    zbuf_ref[0:SUBLANES, :] = z[ts - SUBLANES:ts, :]
    y = cw_ref[0:1, :] * z2 + cw_ref[1:2, :] * z1 + cw_ref[2:3, :] * z + cb_ref[...]
    cy = gb * y
    conv_out = cy * lax.rsqrt(jnp.mean(cy * cy, axis=-1, keepdims=True) + EPS) * cg_ref[...]

    mixed_in = jnp.concatenate(ret_parts + [conv_out], axis=-1).astype(jnp.bfloat16)
    mixed = jnp.dot(mixed_in, w_out_ref[...], preferred_element_type=jnp.float32)
    o_ref[0] = x + gt1 * mixed


def _const_spec(shape):
    nd = len(shape)
    return pl.BlockSpec(shape, lambda b, s: (0,) * nd, pipeline_mode=pl.Buffered(1))


def _mixer(x, pos3, mod3, norm1_g, w_in_bf, inv_freq, conv_w, conv_b, ret_g, conv_g, w_out_bf):
    bsz, seq, d = x.shape
    ts = SEQ_TILE
    return pl.pallas_call(
        _mixer_kernel,
        grid=(bsz, seq // ts),
        in_specs=[
            pl.BlockSpec((1, ts, d), lambda b, s: (b, s, 0)),
            pl.BlockSpec((1, ts, 1), lambda b, s: (b, s, 0)),
            pl.BlockSpec((1, N_ADA, d), lambda b, s: (b, 0, 0)),
            _const_spec((1, d)),
            _const_spec((d, N_IN_COLS)),
            _const_spec((1, HEAD_DIM)),
            _const_spec((CONV_K, CONV_WIDTH)),
            _const_spec((1, CONV_WIDTH)),
            _const_spec((1, RET_WIDTH)),
            _const_spec((1, CONV_WIDTH)),
            _const_spec((d, d)),
        ],
        out_specs=pl.BlockSpec((1, ts, d), lambda b, s: (b, s, 0)),
        out_shape=jax.ShapeDtypeStruct((bsz, seq, d), jnp.float32),
        scratch_shapes=[
            pltpu.VMEM((RET_HEADS, HEAD_DIM, HEAD_DIM), jnp.float32),
            pltpu.VMEM((SUBLANES + ts, CONV_WIDTH), jnp.float32),
            pltpu.VMEM((RET_HEADS, RET_BLOCK, RET_BLOCK), jnp.float32),
            pltpu.VMEM((RET_HEADS, RET_BLOCK, HEAD_DIM), jnp.float32),
            pltpu.VMEM((RET_HEADS, RET_BLOCK, HEAD_DIM), jnp.float32),
        ],
        compiler_params=pltpu.CompilerParams(
            dimension_semantics=("arbitrary", "arbitrary"),
            vmem_limit_bytes=VMEM_LIMIT_BYTES),
        name="mixer",
    )(x, pos3, mod3, norm1_g, w_in_bf, inv_freq, conv_w, conv_b, ret_g, conv_g, w_out_bf)


def _ffn_kernel(x_ref, mod_ref, n2g_ref, w_up_ref, cw_ref, cb_ref, w_down_ref, fg_ref, o_ref,
                p_ref, act_ref):
    s = pl.program_id(1)
    ts = x_ref.shape[1]

    @pl.when(s == 0)
    def _reset():
        p_ref[0:SUBLANES, :] = jnp.zeros((SUBLANES, 2 * D_FF), jnp.float32)

    x = x_ref[0]
    sh2 = mod_ref[0, 3:4, :]
    sc2 = mod_ref[0, 4:5, :]
    gt2 = mod_ref[0, 5:6, :]
    r = lax.rsqrt(jnp.mean(x * x, axis=-1, keepdims=True) + EPS)
    h = ((x * r) * (n2g_ref[...] * (1.0 + sc2)) + sh2).astype(jnp.bfloat16)

    nc = FF_CHUNK
    for c in range(2 * D_FF // nc):
        cs = slice(c * nc, (c + 1) * nc)
        p_ref[SUBLANES:SUBLANES + ts, cs] = jnp.dot(h, w_up_ref[:, cs],
                                                    preferred_element_type=jnp.float32)

    def conv(cs):
        p0 = p_ref[SUBLANES:SUBLANES + ts, cs]
        p1 = p_ref[SUBLANES - 1:SUBLANES - 1 + ts, cs]
        p2 = p_ref[SUBLANES - 2:SUBLANES - 2 + ts, cs]
        return cw_ref[0:1, cs] * p2 + cw_ref[1:2, cs] * p1 + cw_ref[2:3, cs] * p0 + cb_ref[:, cs]

    for c in range(D_FF // nc):
        ua = conv(slice(c * nc, (c + 1) * nc))
        ub = conv(slice(D_FF + c * nc, D_FF + (c + 1) * nc))
        act_ref[:, c * nc:(c + 1) * nc] = (_silu(ua) * ub).astype(jnp.bfloat16)

    p_ref[0:SUBLANES, :] = p_ref[ts:ts + SUBLANES, :]

    down = jnp.dot(act_ref[...], w_down_ref[...], preferred_element_type=jnp.float32)
    x2 = x + gt2 * down
    r2 = lax.rsqrt(jnp.mean(x2 * x2, axis=-1, keepdims=True) + EPS)
    o_ref[0] = (x2 * r2) * fg_ref[...]


def _ffn(x, mod3, norm2_g, w_up_bf, conv_w, conv_b, w_down_bf, final_g):
    bsz, seq, d = x.shape
    ts = SEQ_TILE
    return pl.pallas_call(
        _ffn_kernel,
        grid=(bsz, seq // ts),
        in_specs=[
            pl.BlockSpec((1, ts, d), lambda b, s: (b, s, 0)),
            pl.BlockSpec((1, N_ADA, d), lambda b, s: (b, 0, 0)),
            _const_spec((1, d)),
            _const_spec((d, 2 * D_FF)),
            _const_spec((CONV_K, 2 * D_FF)),
            _const_spec((1, 2 * D_FF)),
            _const_spec((D_FF, d)),
            _const_spec((1, d)),
        ],
        out_specs=pl.BlockSpec((1, ts, d), lambda b, s: (b, s, 0)),
        out_shape=jax.ShapeDtypeStruct((bsz, seq, d), jnp.float32),
        scratch_shapes=[
            pltpu.VMEM((SUBLANES + ts, 2 * D_FF), jnp.float32),
            pltpu.VMEM((ts, D_FF), jnp.bfloat16),
        ],
        compiler_params=pltpu.CompilerParams(
            dimension_semantics=("arbitrary", "arbitrary"),
            vmem_limit_bytes=VMEM_LIMIT_BYTES),
        name="ffn",
    )(x, mod3, norm2_g, w_up_bf, conv_w, conv_b, w_down_bf, final_g)


def kernel(x, c, positions, w_ada, b_ada, norm1_g, w_in, conv_mix_w, conv_mix_b, ret_norm_g,
           conv_norm_g, w_out, norm2_g, w_up, conv_ffn_w, conv_ffn_b, w_down, final_g):
    bsz, seq, d = x.shape
    bf = jnp.bfloat16
    mod3 = _ada(c, w_ada, b_ada).reshape(bsz, N_ADA, d)
    half = jnp.arange(0, HEAD_DIM, 2, dtype=jnp.float32) / HEAD_DIM
    inv_freq = ROPE_BASE ** (-half)
    inv_freq = jnp.concatenate([inv_freq, inv_freq]).reshape(1, HEAD_DIM)
    x1 = _mixer(x, positions.reshape(bsz, seq, 1), mod3, norm1_g.reshape(1, d),
                w_in.astype(bf), inv_freq, conv_mix_w, conv_mix_b.reshape(1, -1),
                ret_norm_g.reshape(1, -1), conv_norm_g.reshape(1, -1), w_out.astype(bf))
    return _ffn(x1, mod3, norm2_g.reshape(1, d), w_up.astype(bf), conv_ffn_w,
                conv_ffn_b.reshape(1, -1), w_down.astype(bf), final_g.reshape(1, d))
```
